```python
import math
import jax, jax.numpy as jnp
from jax import lax
import numpy as np

D_MODEL = 1024
BATCH = 4
SEQ = 8192
DEPTH = 1

CHUNK = 64
QBLK = 128
P_DIM = 256
MLA_HEADS = 8
Q_LORA = 256
KV_LORA = 128
NOPE_D = 64
ROPE_D = 32
V_D = 64
ROPE_THETA = 10000.0
ML_HEADS = 4
ML_HD = 128
ML_W = ML_HEADS * ML_HD
CONV_K = 4
MIX_W = MLA_HEADS * V_D + ML_W
IN_W = Q_LORA + KV_LORA + ROPE_D + ML_W + ML_W + ML_HEADS + ML_HEADS
N_GROUPS = 4
EXPERTS_PER_GROUP = 8
N_EXPERTS = N_GROUPS * EXPERTS_PER_GROUP
TOP_K = 2
D_EXPERT = 256
MOE_BLK = 128
EPS = 1e-6

kernel_name = 'hybrid_mla_mlstm_hier_moe'


def _rmsnorm(x, g):
    xf = x.astype(jnp.float32)
    r = lax.rsqrt(jnp.mean(xf * xf, axis=-1, keepdims=True) + EPS)
    return (xf * r * g.astype(jnp.float32)).astype(x.dtype)


def _split_in(proj):
    sizes = (Q_LORA, KV_LORA, ROPE_D, ML_W, ML_W, ML_HEADS, ML_HEADS)
    offs = np.cumsum(sizes)[:-1].tolist()
    return jnp.split(proj, offs, axis=-1)


def _rope_cs(positions):
    inv = 1.0 / (ROPE_THETA ** (jnp.arange(0, ROPE_D, 2, dtype=jnp.float32) / ROPE_D))
    ang = positions.astype(jnp.float32)[..., None] * inv
    return jnp.cos(ang), jnp.sin(ang)


def _apply_rope(x, cos, sin):
    xf = x.astype(jnp.float32)
    x1, x2 = xf[..., :ROPE_D // 2], xf[..., ROPE_D // 2:]
    return jnp.concatenate([x1 * cos - x2 * sin, x1 * sin + x2 * cos], axis=-1).astype(x.dtype)


def _mla_attention(q_nope, q_rope, k_nope, k_rope, v):
    B, S = q_nope.shape[:2]
    nq = S // QBLK
    scale = 1.0 / math.sqrt(NOPE_D + ROPE_D)
    k_chunk = jnp.arange(S) // CHUNK

    def blocks(t):
        return t.reshape(B, nq, QBLK, MLA_HEADS, t.shape[-1]).transpose(1, 0, 3, 2, 4)

    def one(args):
        qn, qr, blk = args
        s = (jnp.einsum('bhqd,bkhd->bhqk', qn, k_nope)
             + jnp.einsum('bhqr,bkr->bhqk', qr, k_rope)).astype(jnp.float32) * scale
        q_chunk = (blk * QBLK + jnp.arange(QBLK)) // CHUNK
        mask = k_chunk[None, :] <= q_chunk[:, None]
        s = jnp.where(mask, s, -1e30)
        pr = jax.nn.softmax(s, axis=-1).astype(v.dtype)
        return jnp.einsum('bhqk,bkhv->bhqv', pr, v)

    o = lax.map(one, (blocks(q_nope), blocks(q_rope), jnp.arange(nq)))
    return o.transpose(1, 0, 3, 2, 4).reshape(B, S, MLA_HEADS * V_D)


def _causal_conv(x, w, b):
    S = x.shape[1]
    xp = jnp.pad(x, ((0, 0), (CONV_K - 1, 0), (0, 0)))
    return sum(xp[:, k:k + S] * w[k] for k in range(CONV_K)) + b


def _mlstm_chunkwise(q, k, v, i_pre, f_pre):
    B, H, S, dk = q.shape
    dv = v.shape[-1]
    L = CHUNK
    nc = S // L

    def chunks(t):
        return t.reshape(B, H, nc, L, t.shape[-1]).transpose(2, 0, 1, 3, 4).astype(jnp.float32)

    qc, kc, vc = chunks(q), chunks(k), chunks(v)
    ic = i_pre.astype(jnp.float32).reshape(B, H, nc, L).transpose(2, 0, 1, 3)
    logf = jax.nn.log_sigmoid(f_pre.astype(jnp.float32)).reshape(B, H, nc, L).transpose(2, 0, 1, 3)
    bc = jnp.cumsum(logf, axis=-1)
    tril = jnp.tril(jnp.ones((L, L), dtype=bool))

    def step(carry, inp):
        C, n, m = carry
        q_, k_, v_, i_, b_ = inp
        D = b_[..., :, None] - b_[..., None, :] + i_[..., None, :]
        D = jnp.where(tril, D, -jnp.inf)
        inter = b_ + m[..., None]
        m_t = jnp.maximum(jnp.max(D, axis=-1), inter)
        w_intra = jnp.exp(D - m_t[..., None])
        w_inter = jnp.exp(inter - m_t)
        qk = jnp.einsum('bhtd,bhsd->bhts', q_, k_) * w_intra
        num = jnp.einsum('bhts,bhsv->bhtv', qk, v_) + w_inter[..., None] * jnp.einsum('bhtd,bhdv->bhtv', q_, C)
        den = jnp.sum(qk, axis=-1) + w_inter * jnp.einsum('bhtd,bhd->bht', q_, n)
        den = jnp.maximum(jnp.abs(den), jnp.exp(-m_t))
        h = num / den[..., None]
        bL = b_[..., -1]
        w_s = bL[..., None] - b_ + i_
        m_new = jnp.maximum(bL + m, jnp.max(w_s, axis=-1))
        ws = jnp.exp(w_s - m_new[..., None])
        decay = jnp.exp(bL + m - m_new)
        C_new = decay[..., None, None] * C + jnp.einsum('bhs,bhsd,bhsv->bhdv', ws, k_, v_)
        n_new = decay[..., None] * n + jnp.einsum('bhs,bhsd->bhd', ws, k_)
        return (C_new, n_new, m_new), h

    init = (jnp.zeros((B, H, dk, dv), jnp.float32), jnp.zeros((B, H, dk), jnp.float32),
            jnp.zeros((B, H), jnp.float32))
    _, hs = lax.scan(step, init, (qc, kc, vc, ic, bc))
    return hs.transpose(1, 2, 0, 3, 4).reshape(B, H, S, dv)


def _hier_moe(h, w_rg, b_rg, w_re, b_re, w_g, w_u, w_d):
    B, S, D = h.shape
    T = B * S
    hf = h.reshape(T, D)
    g_prob = jax.nn.softmax((hf @ w_rg).astype(jnp.float32) + b_rg.astype(jnp.float32), axis=-1)
    g_top, g_idx = lax.top_k(g_prob, 1)
    e_logits = ((hf @ w_re).astype(jnp.float32) + b_re.astype(jnp.float32)).reshape(T, N_GROUPS, EXPERTS_PER_GROUP)
    e_in = jnp.take_along_axis(e_logits, g_idx[:, :, None], axis=1)[:, 0]
    e_top, e_idx = lax.top_k(jax.nn.softmax(e_in, axis=-1), TOP_K)
    wts = g_top * e_top / jnp.sum(e_top, axis=-1, keepdims=True)
    eid = g_idx * EXPERTS_PER_GROUP + e_idx
    M = T * TOP_K
    P = M + N_EXPERTS * MOE_BLK
    nb = P // MOE_BLK
    flat_e = eid.reshape(M).astype(jnp.int32)
    flat_w = wts.reshape(M)
    flat_tok = jnp.repeat(jnp.arange(T, dtype=jnp.int32), TOP_K)
    order = jnp.argsort(flat_e)
    sorted_e = flat_e[order]
    counts = jnp.zeros((N_EXPERTS,), jnp.int32).at[flat_e].add(1)
    starts = jnp.cumsum(counts) - counts
    padded = (counts + MOE_BLK - 1) // MOE_BLK * MOE_BLK
    pad_ends = jnp.cumsum(padded)
    pad_starts = pad_ends - padded
    dest = pad_starts[sorted_e] + (jnp.arange(M, dtype=jnp.int32) - starts[sorted_e])
    row_tok = jnp.full((P,), T, jnp.int32).at[dest].set(flat_tok[order])
    row_w = jnp.zeros((P,), jnp.float32).at[dest].set(flat_w[order])
    block_e = jnp.minimum(jnp.searchsorted(pad_ends, jnp.arange(nb, dtype=jnp.int32) * MOE_BLK, side='right'),
                          N_EXPERTS - 1)
    x_pad = jnp.concatenate([hf, jnp.zeros((1, D), hf.dtype)], axis=0)
    x_rows = x_pad[row_tok].reshape(nb, MOE_BLK, D)

    def expert(args):
        xb, e = args
        return (jax.nn.silu(xb @ w_g[e]) * (xb @ w_u[e])) @ w_d[e]

    y = lax.map(expert, (x_rows, block_e)).reshape(P, D)
    out = jnp.zeros((T + 1, D), jnp.float32).at[row_tok].add(y.astype(jnp.float32) * row_w[:, None])[:T]
    return out.reshape(B, S, D).astype(h.dtype)


def setup_inputs(seed: int = 0) -> dict:
    key = jax.random.key(seed)
    ks = jax.random.split(key, 40)
    f32 = jnp.float32

    def nrm(k, shape, fan_in):
        return jax.random.normal(k, shape, f32) * (fan_in ** -0.5)

    def gain(k, shape):
        return 1.0 + 0.05 * jax.random.normal(k, shape, f32)

    x = jax.random.normal(ks[0], (BATCH, SEQ, D_MODEL), f32)
    p = jax.random.normal(ks[1], (DEPTH, BATCH, SEQ, P_DIM), f32)
    positions = (jax.random.randint(ks[2], (BATCH, 1), 0, 4096, dtype=jnp.int32)
                 + jnp.arange(SEQ, dtype=jnp.int32)[None, :])
    return {
        'x': x,
        'p': p,
        'positions': positions,
        'norm_mix_g': gain(ks[3], (DEPTH, D_MODEL)),
        'w_in': nrm(ks[4], (DEPTH, D_MODEL, IN_W), D_MODEL),
        'q_norm_g': gain(ks[5], (DEPTH, Q_LORA)),
        'w_uq': nrm(ks[6], (DEPTH, Q_LORA, MLA_HEADS * (NOPE_D + ROPE_D)), Q_LORA),
        'kv_norm_g': gain(ks[7], (DEPTH, KV_LORA)),
        'w_ukv': nrm(ks[8], (DEPTH, KV_LORA, MLA_HEADS * (NOPE_D + V_D)), KV_LORA),
        'conv_w': nrm(ks[9], (DEPTH, CONV_K, ML_W), CONV_K),
        'conv_b': 0.02 * jax.random.normal(ks[10], (DEPTH, ML_W), f32),
        'w_mq': nrm(ks[11], (DEPTH, ML_HEADS, ML_HD, ML_HD), ML_HD),
        'w_mk': nrm(ks[12], (DEPTH, ML_HEADS, ML_HD, ML_HD), ML_HD),
        'w_mv': nrm(ks[13], (DEPTH, ML_HEADS, ML_HD, ML_HD), ML_HD),
        'b_igate': 0.1 * jax.random.normal(ks[14], (DEPTH, ML_HEADS), f32),
        'b_fgate': jnp.linspace(3.0, 6.0, ML_HEADS, dtype=f32)[None, :] + 0.1 * jax.random.normal(ks[15], (DEPTH, ML_HEADS), f32),
        'mh_norm_g': gain(ks[16], (DEPTH, ML_W)),
        'ml_skip': gain(ks[17], (DEPTH, ML_W)),
        'w_o': nrm(ks[18], (DEPTH, MIX_W, D_MODEL), MIX_W),
        'norm_ffn_g': gain(ks[19], (DEPTH, D_MODEL)),
        'w_router_group': nrm(ks[20], (DEPTH, D_MODEL, N_GROUPS), D_MODEL),
        'b_router_group': 0.01 * jax.random.normal(ks[21], (DEPTH, N_GROUPS), f32),
        'w_router_expert': nrm(ks[22], (DEPTH, D_MODEL, N_EXPERTS), D_MODEL),
        'b_router_expert': 0.01 * jax.random.normal(ks[23], (DEPTH, N_EXPERTS), f32),
        'w_gate_e': nrm(ks[24], (DEPTH, N_EXPERTS, D_MODEL, D_EXPERT), D_MODEL),
        'w_up_e': nrm(ks[25], (DEPTH, N_EXPERTS, D_MODEL, D_EXPERT), D_MODEL),
        'w_down_e': nrm(ks[26], (DEPTH, N_EXPERTS, D_EXPERT, D_MODEL), D_EXPERT),
        'norm_ple_g': gain(ks[27], (DEPTH, D_MODEL)),
        'w_ple': nrm(ks[28], (DEPTH, P_DIM, D_MODEL), P_DIM),
        'w_ple_gate': nrm(ks[29], (DEPTH, D_MODEL, D_MODEL), D_MODEL),
        'final_norm_g': gain(ks[30], (D_MODEL,)),
    }


def reference(x, p, positions, norm_mix_g, w_in, q_norm_g, w_uq, kv_norm_g, w_ukv, conv_w, conv_b,
              w_mq, w_mk, w_mv, b_igate, b_fgate, mh_norm_g, ml_skip, w_o, norm_ffn_g,
              w_router_group, b_router_group, w_router_expert, b_router_expert,
              w_gate_e, w_up_e, w_down_e, norm_ple_g, w_ple, w_ple_gate, final_norm_g):
    B, S, _ = x.shape
    cos, sin = _rope_cs(positions)
    for i in range(DEPTH):
        h = _rmsnorm(x, norm_mix_g[i])
        c_q, c_kv, k_r, x_m, z, i_pre, f_pre = _split_in(h @ w_in[i])
        q = (_rmsnorm(c_q, q_norm_g[i]) @ w_uq[i]).reshape(B, S, MLA_HEADS, NOPE_D + ROPE_D)
        kv = (_rmsnorm(c_kv, kv_norm_g[i]) @ w_ukv[i]).reshape(B, S, MLA_HEADS, NOPE_D + V_D)
        q_nope, q_rope = q[..., :NOPE_D], _apply_rope(q[..., NOPE_D:], cos[:, :, None], sin[:, :, None])
        k_nope, v = kv[..., :NOPE_D], kv[..., NOPE_D:]
        k_rope = _apply_rope(k_r, cos, sin)
        attn_out = _mla_attention(q_nope, q_rope, k_nope, k_rope, v)
        xc = jax.nn.silu(_causal_conv(x_m, conv_w[i], conv_b[i]))
        xch = xc.reshape(B, S, ML_HEADS, ML_HD)
        xmh = x_m.reshape(B, S, ML_HEADS, ML_HD)
        mq = jnp.einsum('bshd,hde->bhse', xch, w_mq[i])
        mk = jnp.einsum('bshd,hde->bhse', xch, w_mk[i]) * (ML_HD ** -0.5)
        mv = jnp.einsum('bshd,hde->bhse', xmh, w_mv[i])
        ig = (i_pre + b_igate[i]).transpose(0, 2, 1)
        fg = (f_pre + b_fgate[i]).transpose(0, 2, 1)
        hm = _mlstm_chunkwise(mq, mk, mv, ig, fg).transpose(0, 2, 1, 3)
        mu = jnp.mean(hm, axis=-1, keepdims=True)
        var = jnp.mean(jnp.square(hm - mu), axis=-1, keepdims=True)
        hm = ((hm - mu) * lax.rsqrt(var + EPS)).reshape(B, S, ML_W) * mh_norm_g[i].astype(jnp.float32)
        ml_out = ((hm.astype(x.dtype) + ml_skip[i] * xc) * jax.nn.silu(z)).astype(x.dtype)
        x = x + jnp.concatenate([attn_out, ml_out], axis=-1) @ w_o[i]
        x = x + _hier_moe(_rmsnorm(x, norm_ffn_g[i]), w_router_group[i], b_router_group[i],
                          w_router_expert[i], b_router_expert[i], w_gate_e[i], w_up_e[i], w_down_e[i])
        hp = _rmsnorm(x, norm_ple_g[i])
        x = x + (p[i] @ w_ple[i]) * jax.nn.sigmoid(hp @ w_ple_gate[i])
    return _rmsnorm(x, final_norm_g)
```

```python
import functools
import math

import jax
import jax.numpy as jnp
import numpy as np
from jax import lax
from jax.experimental import pallas as pl
from jax.experimental.pallas import tpu as pltpu

F32 = jnp.float32
BF16 = jnp.bfloat16

D_MODEL = 1024
CHUNK = 64
P_DIM = 256
MLA_HEADS = 8
Q_LORA = 256
KV_LORA = 128
NOPE_D = 64
ROPE_D = 32
V_D = 64
ROPE_THETA = 10000.0
ML_HEADS = 4
ML_HD = 128
ML_W = ML_HEADS * ML_HD
CONV_K = 4
N_GROUPS = 4
EXPERTS_PER_GROUP = 8
N_EXPERTS = N_GROUPS * EXPERTS_PER_GROUP
D_EXPERT = 256
EPS = 1e-6

LANES = 128
HEAD_PAD = LANES
NEG = -1e30
LOG2E = 1.4426950408889634

PROJ_TM = 512
ATTN_TQ = 256
ATTN_TK = 256
MLSTM_TS = 256
MIX_TM = 256
MOE_BLK = 128
DISP_TD = 256
COMB_TC = 256

VMEM_LIMIT = 48 * 1024 * 1024


def _rms(x, g):
    return x * lax.rsqrt(jnp.mean(x * x, axis=-1, keepdims=True) + EPS) * g


def _log_sigmoid(x):
    return jnp.minimum(x, 0.0) - jnp.log1p(jnp.exp(-jnp.abs(x)))


def _params(*sem):
    return pltpu.CompilerParams(dimension_semantics=sem, vmem_limit_bytes=VMEM_LIMIT)


def _proj_kernel(pos_ref, x_ref, g_ref, win_ref, gq_ref, wq_ref, gkv_ref, wkv_ref, invf_ref,
                 q_ref, k_ref, v_ref, xm_ref, z_ref, gate_ref, *, tm):
    x = x_ref[...]
    h = _rms(x, g_ref[...]).astype(BF16)
    proj = jnp.dot(h, win_ref[...], preferred_element_type=F32)
    cq = proj[:, 0:256]
    ckv = proj[:, 256:384]
    kr = proj[:, 384:512]
    krot = proj[:, 512:640]
    xm_ref[...] = proj[:, 640:1152]
    z_ref[...] = proj[:, 1152:1664]
    gate_ref[...] = proj[:, 1664:1792]

    posf = pos_ref[0].astype(F32)
    cols = [jnp.broadcast_to(posf[r:r + 1, :], (LANES, LANES)).T for r in range(tm // LANES)]
    pcol = jnp.concatenate(cols, axis=0)
    ang = pcol * invf_ref[...]
    cs = jnp.cos(ang)
    sn = jnp.sin(ang)

    cqn = _rms(cq, gq_ref[...]).astype(BF16)
    qall = jnp.dot(cqn, wq_ref[...], preferred_element_type=F32)
    ckvn = _rms(ckv, gkv_ref[...]).astype(BF16)
    kv = jnp.dot(ckvn, wkv_ref[...], preferred_element_type=F32)
    krr = kr * cs + krot * sn
    nh = MLA_HEADS
    for hd in range(nh):
        lo, hi = hd * HEAD_PAD, (hd + 1) * HEAD_PAD
        qh = qall[:, lo:hi] * cs + qall[:, nh * HEAD_PAD + lo:nh * HEAD_PAD + hi] * sn
        q_ref[0, hd] = qh.astype(BF16)
        k_ref[0, hd] = (kv[:, lo:hi] + krr).astype(BF16)
    for j in range(nh // 2):
        lo = nh * HEAD_PAD + j * LANES
        v_ref[0, j] = kv[:, lo:lo + LANES].astype(BF16)


def _proj_call(pos3, x2, g, win, gq, wq, gkv, wkv, invf, B, S):
    tm = min(PROJ_TM, S)
    ns = S // tm
    T = B * S
    full = lambda a: pl.BlockSpec(a.shape, lambda b, s: (0,) * a.ndim)
    tok = lambda w: pl.BlockSpec((tm, w), lambda b, s: (b * ns + s, 0))
    return pl.pallas_call(
        functools.partial(_proj_kernel, tm=tm),
        grid=(B, ns),
        in_specs=[pl.BlockSpec((1, tm // LANES, LANES), lambda b, s: (b * ns + s, 0, 0)),
                  tok(D_MODEL), full(g), full(win), full(gq), full(wq), full(gkv), full(wkv), full(invf)],
        out_specs=[pl.BlockSpec((1, MLA_HEADS, tm, HEAD_PAD), lambda b, s: (b, 0, s, 0)),
                   pl.BlockSpec((1, MLA_HEADS, tm, HEAD_PAD), lambda b, s: (b, 0, s, 0)),
                   pl.BlockSpec((1, MLA_HEADS // 2, tm, LANES), lambda b, s: (b, 0, s, 0)),
                   tok(ML_W), tok(ML_W), tok(LANES)],
        out_shape=[jax.ShapeDtypeStruct((B, MLA_HEADS, S, HEAD_PAD), BF16),
                   jax.ShapeDtypeStruct((B, MLA_HEADS, S, HEAD_PAD), BF16),
                   jax.ShapeDtypeStruct((B, MLA_HEADS // 2, S, LANES), BF16),
                   jax.ShapeDtypeStruct((T, ML_W), F32),
                   jax.ShapeDtypeStruct((T, ML_W), F32),
                   jax.ShapeDtypeStruct((T, LANES), F32)],
        compiler_params=_params("arbitrary", "arbitrary"),
        name="proj",
    )(pos3, x2, g, win, gq, wq, gkv, wkv, invf)


def _attn_kernel(q_ref, k_ref, v_ref, o_ref, *, tq, tk):
    qi = pl.program_id(2)
    c = LOG2E / math.sqrt(NOPE_D + ROPE_D)
    sub = tq // tk
    outs = []
    for hh in range(2):
        q = q_ref[0, hh]

        def tile(kj, carry, masked, hh=hh, q=q):
            m, l, acc = carry
            ks = pl.multiple_of(kj * tk, tk)
            k = k_ref[0, hh, pl.ds(ks, tk), :]
            v = v_ref[0, 0, pl.ds(ks, tk), :]
            s = lax.dot_general(q, k, (((1,), (1,)), ((), ())), preferred_element_type=F32) * c
            if masked:
                qc = (qi * tq + lax.broadcasted_iota(jnp.int32, (tq, tk), 0)) // CHUNK
                kc = (kj * tk + lax.broadcasted_iota(jnp.int32, (tq, tk), 1)) // CHUNK
                s = jnp.where(kc <= qc, s, NEG)
            m_new = jnp.maximum(m, jnp.max(s, axis=-1, keepdims=True))
            alpha = jnp.exp2(m - m_new)
            p = jnp.exp2(s - m_new)
            l = alpha * l + jnp.sum(p, axis=-1, keepdims=True)
            acc = alpha * acc + jnp.dot(p.astype(BF16), v, preferred_element_type=F32)
            return m_new, l, acc

        init = (jnp.full((tq, 1), NEG, F32), jnp.zeros((tq, 1), F32), jnp.zeros((tq, LANES), F32))
        carry = lax.fori_loop(0, qi * sub, lambda kj, cr: tile(kj, cr, False), init)
        for d in range(sub):
            carry = tile(qi * sub + d, carry, True)
        _, l, acc = carry
        outs.append(acc / l)
    lane = lax.broadcasted_iota(jnp.int32, (tq, LANES), 1)
    o_ref[0] = jnp.where(lane < V_D, outs[0], outs[1]).astype(BF16)


def _attn_call(q, k, v, B, S):
    tq = min(ATTN_TQ, S)
    tk = min(ATTN_TK, tq)
    return pl.pallas_call(
        functools.partial(_attn_kernel, tq=tq, tk=tk),
        grid=(B, MLA_HEADS // 2, S // tq),
        in_specs=[pl.BlockSpec((1, 2, tq, HEAD_PAD), lambda b, j, i: (b, j, i, 0)),
                  pl.BlockSpec((1, 2, S, HEAD_PAD), lambda b, j, i: (b, j, 0, 0)),
                  pl.BlockSpec((1, 1, S, LANES), lambda b, j, i: (b, j, 0, 0))],
        out_specs=pl.BlockSpec((1, tq, LANES), lambda b, j, i: (b, i, j)),
        out_shape=jax.ShapeDtypeStruct((B, S, MLA_HEADS * V_D), BF16),
        compiler_params=_params("arbitrary", "arbitrary", "arbitrary"),
        name="attn",
    )(q, k, v)


def _chunk_cumsum(x, axis):
    idx = lax.broadcasted_iota(jnp.int32, x.shape, axis) % CHUNK
    sh = 1
    while sh < CHUNK:
        x = x + jnp.where(idx >= sh, pltpu.roll(x, sh, axis), 0.0)
        sh *= 2
    return x


def _mlstm_kernel(xm_ref, z_ref, gc_ref, gr_ref, cw_ref, cb_ref, wq_ref, wk_ref, wv_ref,
                  bcol_ref, brow_ref, ng_ref, sk_ref, o_ref, c_ref, n_ref, m_ref, tail_ref, *, ts):
    si = pl.program_id(1)

    @pl.when(si == 0)
    def _():
        c_ref[...] = jnp.zeros_like(c_ref)
        n_ref[...] = jnp.zeros_like(n_ref)
        m_ref[...] = jnp.zeros_like(m_ref)
        tail_ref[...] = jnp.zeros_like(tail_ref)

    L = CHUNK
    xm = xm_ref[...]
    ext = jnp.concatenate([tail_ref[...], xm], axis=0)
    cw = cw_ref[...]
    conv = ext[5:5 + ts] * cw[0:1]
    for kk in range(1, CONV_K):
        conv = conv + ext[5 + kk:5 + kk + ts] * cw[kk:kk + 1]
    conv = conv + cb_ref[...]
    tail_ref[...] = xm[ts - 8:ts]
    xc = conv * jax.nn.sigmoid(conv)
    zz = z_ref[...]
    zg = zz * jax.nn.sigmoid(zz)

    gc = gc_ref[...] + bcol_ref[...]
    gr = gr_ref[0] + brow_ref[...]
    bc_all = _chunk_cumsum(_log_sigmoid(gc), 0)
    br_all = _chunk_cumsum(_log_sigmoid(gr), 1)
    tril = lax.broadcasted_iota(jnp.int32, (L, L), 0) >= lax.broadcasted_iota(jnp.int32, (L, L), 1)

    for hd in range(ML_HEADS):
        lo, hi = hd * ML_HD, (hd + 1) * ML_HD
        xch = xc[:, lo:hi]
        xcb = xch.astype(BF16)
        qa = jnp.dot(xcb, wq_ref[hd], preferred_element_type=F32)
        ka = jnp.dot(xcb, wk_ref[hd], preferred_element_type=F32) * (ML_HD ** -0.5)
        va = jnp.dot(xm[:, lo:hi].astype(BF16), wv_ref[hd], preferred_element_type=F32)
        C = c_ref[hd]
        n = n_ref[hd]
        m = m_ref[hd][:, 0:1]
        hs = []
        for ci in range(ts // L):
            r0, r1 = ci * L, (ci + 1) * L
            bcol = bc_all[r0:r1, ML_HEADS + hd:ML_HEADS + hd + 1]
            icol = gc[r0:r1, hd:hd + 1]
            brow = br_all[ML_HEADS + hd:ML_HEADS + hd + 1, r0:r1]
            irow = gr[hd:hd + 1, r0:r1]
            q_, k_, v_ = qa[r0:r1], ka[r0:r1], va[r0:r1]
            qb, kb, vb = q_.astype(BF16), k_.astype(BF16), v_.astype(BF16)
            D = jnp.where(tril, bcol - brow + irow, -jnp.inf)
            inter = bcol + m
            m_t = jnp.maximum(jnp.max(D, axis=-1, keepdims=True), inter)
            w_intra = jnp.exp(D - m_t)
            w_inter = jnp.exp(inter - m_t)
            qk = lax.dot_general(qb, kb, (((1,), (1,)), ((), ())), preferred_element_type=F32) * w_intra
            num = (jnp.dot(qk.astype(BF16), vb, preferred_element_type=F32)
                   + w_inter * jnp.dot(qb, C.astype(BF16), preferred_element_type=F32))
            den = jnp.sum(qk, axis=-1, keepdims=True) + w_inter * jnp.sum(q_ * n, axis=-1, keepdims=True)
            den = jnp.maximum(jnp.abs(den), jnp.exp(-m_t))
            hs.append(num / den)
            bL = bcol[L - 1:L, :]
            ws_col = bL - bcol + icol
            m_new = jnp.maximum(bL + m, jnp.max(ws_col, axis=0, keepdims=True))
            ws = jnp.exp(ws_col - m_new)
            decay = jnp.exp(bL + m - m_new)
            kw = k_ * ws
            C = decay * C + lax.dot_general(kw.astype(BF16), vb, (((0,), (0,)), ((), ())),
                                            preferred_element_type=F32)
            n = decay * n + jnp.sum(kw, axis=0, keepdims=True)
            m = m_new
        c_ref[hd] = C
        n_ref[hd] = n
        m_ref[hd] = jnp.broadcast_to(m, (1, LANES))
        hm = jnp.concatenate(hs, axis=0)
        mu = jnp.mean(hm, axis=-1, keepdims=True)
        dlt = hm - mu
        var = jnp.mean(dlt * dlt, axis=-1, keepdims=True)
        hn = dlt * lax.rsqrt(var + EPS) * ng_ref[:, lo:hi]
        o_ref[:, lo:hi] = ((hn + sk_ref[:, lo:hi] * xch) * zg[:, lo:hi]).astype(BF16)


def _mlstm_call(xm, z, gc, gr, cw, cb, wq, wk, wv, bcol, brow, ng, sk, B, S):
    ts = min(MLSTM_TS, S)
    ns = S // ts
    T = B * S
    full = lambda a: pl.BlockSpec(a.shape, lambda b, s: (0,) * a.ndim)
    tok = lambda w: pl.BlockSpec((ts, w), lambda b, s: (b * ns + s, 0))
    return pl.pallas_call(
        functools.partial(_mlstm_kernel, ts=ts),
        grid=(B, ns),
        in_specs=[tok(ML_W), tok(ML_W), tok(LANES),
                  pl.BlockSpec((1, 8, ts), lambda b, s: (b, 0, s)),
                  full(cw), full(cb), full(wq), full(wk), full(wv), full(bcol), full(brow), full(ng), full(sk)],
        out_specs=tok(ML_W),
        out_shape=jax.ShapeDtypeStruct((T, ML_W), BF16),
        scratch_shapes=[pltpu.VMEM((ML_HEADS, ML_HD, ML_HD), F32),
                        pltpu.VMEM((ML_HEADS, 1, ML_HD), F32),
                        pltpu.VMEM((ML_HEADS, 1, LANES), F32),
                        pltpu.VMEM((8, ML_W), F32)],
        compiler_params=_params("arbitrary", "arbitrary"),
        name="mlstm",
    )(xm, z, gc, gr, cw, cb, wq, wk, wv, bcol, brow, ng, sk)


def _mix_kernel(x_ref, a_ref, ml_ref, woa_ref, wom_ref, g_ref, wr_ref, br_ref,
                x1_ref, h2_ref, route_ref, cnt_ref, *, tm):
    i = pl.program_id(0)

    @pl.when(i == 0)
    def _():
        cnt_ref[...] = jnp.zeros_like(cnt_ref)

    x1 = (x_ref[...] + jnp.dot(a_ref[...], woa_ref[...], preferred_element_type=F32)
          + jnp.dot(ml_ref[...], wom_ref[...], preferred_element_type=F32))
    x1_ref[...] = x1
    h2 = _rms(x1, g_ref[...])
    h2_ref[...] = h2
    logits = jnp.dot(h2.astype(BF16), wr_ref[...], preferred_element_type=F32) + br_ref[...]
    lane = lax.broadcasted_iota(jnp.int32, (tm, LANES), 1)
    big = jnp.int32(LANES)

    gmask = (lane >= N_EXPERTS) & (lane < N_EXPERTS + N_GROUPS)
    gl = jnp.where(gmask, logits, NEG)
    gmax = jnp.max(gl, axis=-1, keepdims=True)
    gsum = jnp.sum(jnp.exp(gl - gmax), axis=-1, keepdims=True)
    g_top = 1.0 / gsum
    g_idx = jnp.min(jnp.where(gmask & (gl == gmax), lane, big), axis=-1, keepdims=True) - N_EXPERTS

    emask = (lane < N_EXPERTS) & ((lane // EXPERTS_PER_GROUP) == g_idx)
    el = jnp.where(emask, logits, NEG)
    emax = jnp.max(el, axis=-1, keepdims=True)
    eexp = jnp.where(emask, jnp.exp(el - emax), 0.0)
    eprob = eexp / jnp.sum(eexp, axis=-1, keepdims=True)
    pe = jnp.where(emask, eprob, -1.0)
    p1 = jnp.max(pe, axis=-1, keepdims=True)
    i1 = jnp.min(jnp.where(pe == p1, lane, big), axis=-1, keepdims=True)
    pe2 = jnp.where(lane == i1, -1.0, pe)
    p2 = jnp.max(pe2, axis=-1, keepdims=True)
    i2 = jnp.min(jnp.where(pe2 == p2, lane, big), axis=-1, keepdims=True)
    psum = p1 + p2
    w1 = g_top * p1 / psum
    w2 = g_top * p2 / psum

    oh1 = lane == i1
    oh2 = lane == i2
    onehot = jnp.where(oh1 | oh2, 1.0, 0.0)
    stril = (lax.broadcasted_iota(jnp.int32, (tm, tm), 0) > lax.broadcasted_iota(jnp.int32, (tm, tm), 1))
    before = jnp.dot(jnp.where(stril, 1.0, 0.0).astype(BF16), onehot.astype(BF16),
                     preferred_element_type=F32) + cnt_ref[0:1, :]
    r1 = jnp.sum(jnp.where(oh1, before, 0.0), axis=-1, keepdims=True)
    r2 = jnp.sum(jnp.where(oh2, before, 0.0), axis=-1, keepdims=True)
    cnt_ref[...] = cnt_ref[...] + jnp.sum(onehot, axis=0, keepdims=True)

    out = jnp.where(lane == 0, i1.astype(F32), 0.0)
    out = jnp.where(lane == 1, i2.astype(F32), out)
    out = jnp.where(lane == 2, r1, out)
    out = jnp.where(lane == 3, r2, out)
    out = jnp.where(lane == 4, w1, out)
    out = jnp.where(lane == 5, w2, out)
    route_ref[...] = out


def _mix_call(x2, attn, ml, woa, wom, g, wr, br):
    T = x2.shape[0]
    tm = min(MIX_TM, T)
    full = lambda a: pl.BlockSpec(a.shape, lambda i: (0,) * a.ndim)
    tok = lambda w: pl.BlockSpec((tm, w), lambda i: (i, 0))
    return pl.pallas_call(
        functools.partial(_mix_kernel, tm=tm),
        grid=(T // tm,),
        in_specs=[tok(D_MODEL), tok(MLA_HEADS * V_D), tok(ML_W), full(woa), full(wom), full(g), full(wr), full(br)],
        out_specs=[tok(D_MODEL), tok(D_MODEL), tok(LANES), pl.BlockSpec((8, LANES), lambda i: (0, 0))],
        out_shape=[jax.ShapeDtypeStruct((T, D_MODEL), F32),
                   jax.ShapeDtypeStruct((T, D_MODEL), F32),
                   jax.ShapeDtypeStruct((T, LANES), F32),
                   jax.ShapeDtypeStruct((8, LANES), F32)],
        compiler_params=_params("arbitrary"),
        name="mix",
    )(x2, attn, ml, woa, wom, g, wr, br)


def _row_copy(src_ref, s, dst_ref, d, sem):
    return pltpu.make_async_copy(src_ref.at[pl.ds(s, 1)], dst_ref.at[pl.ds(d, 1)], sem)


def _dispatch_kernel(valid_ref, dest_ref, h_ref, xs_ref, zbuf, sem, zsem, *, td, blk, nb):
    @pl.when(pl.program_id(0) == 0)
    def _():
        zbuf[...] = jnp.zeros_like(zbuf)

        def zero_copy(j):
            return pltpu.make_async_copy(zbuf, xs_ref.at[pl.ds(pl.multiple_of(j * blk, blk), blk)], zsem)

        def zissue(j, carry):
            @pl.when(valid_ref[j] < blk)
            def _():
                zero_copy(j).start()
            return carry

        def zdrain(j, carry):
            @pl.when(valid_ref[j] < blk)
            def _():
                zero_copy(j).wait()
            return carry

        lax.fori_loop(0, nb, zissue, 0)
        lax.fori_loop(0, nb, zdrain, 0)

    def issue(t, carry):
        for kk in range(2):
            _row_copy(h_ref, t, xs_ref, dest_ref[0, 0, 2 * t + kk], sem).start()
        return carry

    lax.fori_loop(0, td, issue, 0)

    def drain(t, carry):
        for kk in range(2):
            _row_copy(h_ref, t, xs_ref, dest_ref[0, 0, 2 * t + kk], sem).wait()
        return carry

    lax.fori_loop(0, td, drain, 0)


def _dispatch_call(valid, dest3, h2, P):
    T = h2.shape[0]
    td = dest3.shape[2] // 2
    blk = MOE_BLK
    grid_spec = pltpu.PrefetchScalarGridSpec(
        num_scalar_prefetch=1,
        grid=(T // td,),
        in_specs=[pl.BlockSpec((1, 1, 2 * td), lambda i, vl: (i, 0, 0), memory_space=pltpu.SMEM),
                  pl.BlockSpec((td, D_MODEL), lambda i, vl: (i, 0))],
        out_specs=pl.BlockSpec(memory_space=pl.ANY),
        scratch_shapes=[pltpu.VMEM((blk, D_MODEL), F32), pltpu.SemaphoreType.DMA(()),
                        pltpu.SemaphoreType.DMA(())],
    )
    return pl.pallas_call(
        functools.partial(_dispatch_kernel, td=td, blk=blk, nb=P // blk),
        grid_spec=grid_spec,
        out_shape=jax.ShapeDtypeStruct((P, D_MODEL), F32),
        compiler_params=_params("arbitrary"),
        name="dispatch",
    )(valid, dest3, h2)


def _expert_kernel(be_ref, valid_ref, xs_ref, wg_ref, wu_ref, wd_ref, y_ref, *, blk):
    j = pl.program_id(0)
    nvalid = valid_ref[j]

    @pl.when(nvalid > 0)
    def _():
        x = xs_ref[...].astype(BF16)
        g = jnp.dot(x, wg_ref[0], preferred_element_type=F32)
        u = jnp.dot(x, wu_ref[0], preferred_element_type=F32)
        a = (g * jax.nn.sigmoid(g) * u).astype(BF16)
        y_ref[...] = jnp.dot(a, wd_ref[0], preferred_element_type=F32)

    @pl.when(nvalid <= 0)
    def _():
        y_ref[...] = jnp.zeros_like(y_ref)


def _expert_call(block_e, valid, last_blk, xs, wg, wu, wd):
    P = xs.shape[0]
    blk = MOE_BLK
    nb = P // blk
    grid_spec = pltpu.PrefetchScalarGridSpec(
        num_scalar_prefetch=3,
        grid=(nb,),
        in_specs=[pl.BlockSpec((blk, D_MODEL), lambda j, be, vl, lb: (jnp.minimum(j, lb[0]), 0)),
                  pl.BlockSpec((1, D_MODEL, D_EXPERT), lambda j, be, vl, lb: (be[j], 0, 0)),
                  pl.BlockSpec((1, D_MODEL, D_EXPERT), lambda j, be, vl, lb: (be[j], 0, 0)),
                  pl.BlockSpec((1, D_EXPERT, D_MODEL), lambda j, be, vl, lb: (be[j], 0, 0))],
        out_specs=pl.BlockSpec((blk, D_MODEL), lambda j, be, vl, lb: (j, 0)),
    )

    def kern(be_ref, valid_ref, lb_ref, xs_ref, wg_ref, wu_ref, wd_ref, y_ref):
        _expert_kernel(be_ref, valid_ref, xs_ref, wg_ref, wu_ref, wd_ref, y_ref, blk=blk)

    return pl.pallas_call(
        kern,
        grid_spec=grid_spec,
        out_shape=jax.ShapeDtypeStruct((P, D_MODEL), F32),
        compiler_params=_params("arbitrary"),
        name="experts",
    )(block_e, valid, last_blk, xs, wg, wu, wd)


def _combine_kernel(dest_ref, x1_ref, route_ref, p_ref, wple_ref, wpg_ref, gple_ref, gfin_ref, y_ref,
                    o_ref, ybuf, sem, *, tc):
    def issue(t, carry):
        for kk in range(2):
            _row_copy(y_ref, dest_ref[0, 0, 2 * t + kk], ybuf.at[kk], t, sem).start()
        return carry

    lax.fori_loop(0, tc, issue, 0)
    ple = jnp.dot(p_ref[...].astype(BF16), wple_ref[...], preferred_element_type=F32)

    def drain(t, carry):
        for kk in range(2):
            _row_copy(y_ref, dest_ref[0, 0, 2 * t + kk], ybuf.at[kk], t, sem).wait()
        return carry

    lax.fori_loop(0, tc, drain, 0)
    route = route_ref[...]
    w1 = route[:, 4:5]
    w2 = route[:, 5:6]
    x2 = x1_ref[...] + (ybuf[0] * w1 + ybuf[1] * w2)
    hp = _rms(x2, gple_ref[...])
    gate = jax.nn.sigmoid(jnp.dot(hp.astype(BF16), wpg_ref[...], preferred_element_type=F32))
    x3 = x2 + ple * gate
    o_ref[...] = _rms(x3, gfin_ref[...])


def _combine_call(dest3, x1, route, p2, wple, wpg, gple, gfin, y):
    T = x1.shape[0]
    tc = dest3.shape[2] // 2
    full = lambda a: pl.BlockSpec(a.shape, lambda i: (0,) * a.ndim)
    tok = lambda w: pl.BlockSpec((tc, w), lambda i: (i, 0))
    return pl.pallas_call(
        functools.partial(_combine_kernel, tc=tc),
        grid=(T // tc,),
        in_specs=[pl.BlockSpec((1, 1, 2 * tc), lambda i: (i, 0, 0), memory_space=pltpu.SMEM),
                  tok(D_MODEL), tok(LANES), tok(P_DIM), full(wple), full(wpg), full(gple), full(gfin),
                  pl.BlockSpec(memory_space=pl.ANY)],
        out_specs=tok(D_MODEL),
        out_shape=jax.ShapeDtypeStruct((T, D_MODEL), F32),
        scratch_shapes=[pltpu.VMEM((2, tc, D_MODEL), F32), pltpu.SemaphoreType.DMA(())],
        compiler_params=_params("arbitrary"),
        name="combine",
    )(dest3, x1, route, p2, wple, wpg, gple, gfin, y)


def _rot_half_cols(w):
    half = ROPE_D // 2
    return jnp.concatenate([-w[:, half:], w[:, :half]], axis=1)


def _pad_cols(w, lo, width):
    out = jnp.zeros((w.shape[0], width), w.dtype)
    return out.at[:, lo:lo + w.shape[1]].set(w)


def _layer_weights(w_in, w_uq, w_ukv):
    o = np.cumsum([0, Q_LORA, KV_LORA, ROPE_D, ML_W, ML_W, ML_HEADS, ML_HEADS])
    w_cq, w_ckv, w_kr, w_xm, w_z = (w_in[:, o[i]:o[i + 1]] for i in range(5))
    w_gate = w_in[:, o[5]:o[7]]
    win = jnp.concatenate([w_cq, w_ckv,
                           _pad_cols(w_kr, NOPE_D, LANES), _pad_cols(_rot_half_cols(w_kr), NOPE_D, LANES),
                           w_xm, w_z, _pad_cols(w_gate, 0, LANES)], axis=1).astype(BF16)
    dq = NOPE_D + ROPE_D
    q_blocks, qr_blocks, k_blocks, v_blocks = [], [], [], []
    for h in range(MLA_HEADS):
        wq_h = w_uq[:, h * dq:(h + 1) * dq]
        q_blocks.append(_pad_cols(wq_h, 0, HEAD_PAD))
        qr_blocks.append(_pad_cols(_rot_half_cols(wq_h[:, NOPE_D:]), NOPE_D, HEAD_PAD))
        wkv_h = w_ukv[:, h * (NOPE_D + V_D):(h + 1) * (NOPE_D + V_D)]
        k_blocks.append(_pad_cols(wkv_h[:, :NOPE_D], 0, HEAD_PAD))
        v_blocks.append(wkv_h[:, NOPE_D:])
    wq = jnp.concatenate(q_blocks + qr_blocks, axis=1).astype(BF16)
    wkv = jnp.concatenate(k_blocks + v_blocks, axis=1).astype(BF16)
    return win, wq, wkv


def kernel(x, p, positions, norm_mix_g, w_in, q_norm_g, w_uq, kv_norm_g, w_ukv, conv_w, conv_b, w_mq, w_mk, w_mv, b_igate, b_fgate, mh_norm_g, ml_skip, w_o, norm_ffn_g, w_router_group, b_router_group, w_router_expert, b_router_expert, w_gate_e, w_up_e, w_down_e, norm_ple_g, w_ple, w_ple_gate, final_norm_g):
    B, S, D = x.shape
    T = B * S
    depth = w_in.shape[0]
    assert D == D_MODEL and S % CHUNK == 0 and S % LANES == 0
    assert depth == 1, "the final rmsnorm is fused into the layer's last kernel"

    inv = 1.0 / (ROPE_THETA ** (jnp.arange(0, ROPE_D, 2, dtype=F32) / ROPE_D))
    invf = jnp.zeros((1, LANES), F32).at[0, NOPE_D:NOPE_D + ROPE_D].set(jnp.concatenate([inv, inv]))
    tm = min(PROJ_TM, S)
    pos3 = positions.astype(jnp.int32).reshape(T // tm, tm // LANES, LANES)
    row = lambda v: v.reshape(1, -1).astype(F32)

    x2 = x.reshape(T, D)
    for i in range(depth):
        win, wq, wkv = _layer_weights(w_in[i], w_uq[i], w_ukv[i])
        q, k, v, xm, z, gates = _proj_call(pos3, x2, row(norm_mix_g[i]), win, row(q_norm_g[i]), wq,
                                           row(kv_norm_g[i]), wkv, invf, B, S)
        attn = _attn_call(q, k, v, B, S).reshape(T, MLA_HEADS * V_D)

        gbias = jnp.concatenate([b_igate[i], b_fgate[i]]).astype(F32)
        bcol = jnp.zeros((1, LANES), F32).at[0, :2 * ML_HEADS].set(gbias)
        brow = gbias.reshape(2 * ML_HEADS, 1)
        gr = gates[:, :2 * ML_HEADS].reshape(B, S, 2 * ML_HEADS).transpose(0, 2, 1)
        ml = _mlstm_call(xm, z, gates, gr, conv_w[i].astype(F32), row(conv_b[i]),
                         w_mq[i].astype(BF16), w_mk[i].astype(BF16), w_mv[i].astype(BF16),
                         bcol, brow, row(mh_norm_g[i]), row(ml_skip[i]), B, S)

        nattn = MLA_HEADS * V_D
        wr = jnp.zeros((D, LANES), F32).at[:, :N_EXPERTS].set(w_router_expert[i])
        wr = wr.at[:, N_EXPERTS:N_EXPERTS + N_GROUPS].set(w_router_group[i]).astype(BF16)
        br = jnp.zeros((1, LANES), F32).at[0, :N_EXPERTS].set(b_router_expert[i])
        br = br.at[0, N_EXPERTS:N_EXPERTS + N_GROUPS].set(b_router_group[i])
        x1, h2, route, cnt = _mix_call(x2, attn, ml, w_o[i][:nattn].astype(BF16), w_o[i][nattn:].astype(BF16),
                                       row(norm_ffn_g[i]), wr, br)

        blk = MOE_BLK
        P = 2 * T + N_EXPERTS * blk
        nb = P // blk
        counts = cnt[0, :N_EXPERTS].astype(jnp.int32)
        padded = (counts + blk - 1) // blk * blk
        pad_ends = jnp.cumsum(padded)
        pad_starts = pad_ends - padded
        eid = route[:, 0:2].astype(jnp.int32)
        rank = route[:, 2:4].astype(jnp.int32)
        dest = pad_starts[eid] + rank
        blk_start = jnp.arange(nb, dtype=jnp.int32) * blk
        block_e = jnp.minimum(jnp.searchsorted(pad_ends, blk_start, side='right'), N_EXPERTS - 1).astype(jnp.int32)
        seg_end = pad_starts[block_e] + counts[block_e]
        valid = jnp.clip(seg_end - blk_start, 0, blk).astype(jnp.int32)
        valid = jnp.where(blk_start < pad_ends[-1], valid, 0)
        last_blk = jnp.maximum(pad_ends[-1] // blk - 1, 0).astype(jnp.int32).reshape(1)

        td = min(DISP_TD, T)
        xs = _dispatch_call(valid, dest.reshape(T // td, 1, 2 * td), h2, P)
        y = _expert_call(block_e, valid, last_blk, xs, w_gate_e[i].astype(BF16), w_up_e[i].astype(BF16),
                         w_down_e[i].astype(BF16))
        tc = min(COMB_TC, T)
        x2 = _combine_call(dest.reshape(T // tc, 1, 2 * tc), x1, route, p[i].reshape(T, P_DIM),
                           w_ple[i].astype(BF16), w_ple_gate[i].astype(BF16), row(norm_ple_g[i]),
                           row(final_norm_g), y)
    return x2.reshape(B, S, D)
```
